```python
import math
import jax, jax.numpy as jnp
from jax import lax
import numpy as np

D_MODEL = 4096
BATCH = 2
SEQ = 4096
DEPTH = 2

CHUNK = 64
Q_BLOCK = 128
ROPE_THETA = 10000.0
EPS = 1e-6
NEG = -1e30

HEAD_DIM = 128
A_HEADS = D_MODEL // 512
A_WIDTH = A_HEADS * HEAD_DIM
IDX_HEADS = 32
IDX_DIM = 64
TOPK_MAX = 256
B_HEADS = D_MODEL // 256
B_NOPE = 128
B_ROPE = 64
B_V = 128
B_QLORA = D_MODEL // 4
B_KVLORA = D_MODEL // 8
B_WIDTH = B_HEADS * B_V
C_HEADS = D_MODEL // 512
C_WIDTH = C_HEADS * HEAD_DIM

MIX_WIDTH = A_WIDTH + B_WIDTH + C_WIDTH
D_FF = 4 * D_MODEL
N_MOD = 6
IN_SPLITS = (A_WIDTH, A_WIDTH, A_WIDTH, IDX_HEADS * IDX_DIM, IDX_DIM, IDX_HEADS,
             B_QLORA, B_KVLORA, B_ROPE,
             C_WIDTH, C_WIDTH, C_WIDTH, C_HEADS)
IN_WIDTH = sum(IN_SPLITS)

kernel_name = 'hybrid_dsa_mla_fox_chunk_encoder'


def _rmsnorm(x, g):
    xf = x.astype(jnp.float32)
    y = xf * lax.rsqrt(jnp.mean(xf * xf, axis=-1, keepdims=True) + EPS)
    return (y * g.astype(jnp.float32)).astype(x.dtype)


def _rope(x, pos):
    d = x.shape[-1]
    inv_freq = jnp.exp(jnp.arange(0, d, 2, dtype=jnp.float32) * (-math.log(ROPE_THETA) / d))
    ang = pos.astype(jnp.float32)[..., None] * inv_freq
    cos = jnp.cos(ang)[:, :, None, :]
    sin = jnp.sin(ang)[:, :, None, :]
    xf = x.astype(jnp.float32)
    x1, x2 = xf[..., : d // 2], xf[..., d // 2:]
    return jnp.concatenate([x1 * cos - x2 * sin, x2 * cos + x1 * sin], axis=-1).astype(x.dtype)


def _to_blocks(a):
    b, s = a.shape[0], a.shape[1]
    return jnp.moveaxis(a.reshape((b, s // Q_BLOCK, Q_BLOCK) + a.shape[2:]), 1, 0)


def _from_blocks(a):
    a = jnp.moveaxis(a, 0, 1)
    return a.reshape((a.shape[0], a.shape[1] * a.shape[2]) + a.shape[3:])


def _dense_attention(q, k, v, log_decay_cum=None):
    s_len, dk = q.shape[1], q.shape[-1]
    n_blk = s_len // Q_BLOCK
    scale = dk ** -0.5
    key_pos = jnp.arange(s_len)
    use_decay = log_decay_cum is not None
    xs = (_to_blocks(q), jnp.arange(n_blk))
    if use_decay:
        xs = xs + (_to_blocks(log_decay_cum),)
        cum_k = jnp.swapaxes(log_decay_cum, 1, 2)

    def block(args):
        q_i, i = args[0], args[1]
        t = i * Q_BLOCK + jnp.arange(Q_BLOCK)
        logits = jnp.einsum('bqhd,bshd->bhqs', q_i, k).astype(jnp.float32) * scale
        if use_decay:
            cum_q = jnp.swapaxes(args[2], 1, 2)
            logits = logits + cum_q[..., None] - cum_k[:, :, None, :]
            mask = key_pos[None, :] <= t[:, None]
        else:
            mask = (key_pos // CHUNK)[None, :] <= (t // CHUNK)[:, None]
        logits = jnp.where(mask, logits, NEG)
        p = jax.nn.softmax(logits, axis=-1)
        return jnp.einsum('bhqs,bshd->bqhd', p.astype(v.dtype), v)

    return _from_blocks(lax.map(block, xs))


def _dsa_attention(q, k, v, iq, ik, iw):
    s_len, d = q.shape[1], q.shape[-1]
    n_blk = s_len // Q_BLOCK
    k_sel = min(TOPK_MAX, s_len // 4)
    kv = jnp.concatenate([k, v], axis=-1)
    key_chunk = jnp.arange(s_len) // CHUNK
    idx_scale = IDX_DIM ** -0.5 * IDX_HEADS ** -0.5
    scale = d ** -0.5

    def block(args):
        q_i, iq_i, iw_i, i = args
        t = i * Q_BLOCK + jnp.arange(Q_BLOCK)
        q_chunk = t // CHUNK
        admissible = key_chunk[None, :] <= q_chunk[:, None]
        rel = jax.nn.relu(jnp.einsum('bqhd,bsd->bqhs', iq_i, ik).astype(jnp.float32))
        score = jnp.einsum('bqhs,bqh->bqs', rel, iw_i.astype(jnp.float32)) * idx_scale
        score = jnp.where(admissible[None], score, NEG)
        _, sel = lax.top_k(score, k_sel)
        kv_sel = jax.vmap(lambda a, ii: a[ii])(kv, sel)
        k_s, v_s = kv_sel[..., :d], kv_sel[..., d:]
        logits = jnp.einsum('bqhd,bqkhd->bhqk', q_i, k_s).astype(jnp.float32) * scale
        valid = (sel // CHUNK) <= q_chunk[None, :, None]
        logits = jnp.where(valid[:, None], logits, NEG)
        p = jax.nn.softmax(logits, axis=-1)
        return jnp.einsum('bhqk,bqkhd->bqhd', p.astype(v.dtype), v_s)

    xs = (_to_blocks(q), _to_blocks(iq), _to_blocks(iw), jnp.arange(n_blk))
    return _from_blocks(lax.map(block, xs))


def _mixer(h, positions, w_in, g_cq, g_ckv, w_uq, w_ukv, b_f, g_out_a, g_out_b, g_out_c, w_out):
    bsz, s_len, _ = h.shape
    proj = h @ w_in
    (qa, ka, va, iq, ik, iw, cq, ckv, kr, qc, kc, vc, fl) = jnp.split(
        proj, np.cumsum(IN_SPLITS)[:-1].tolist(), axis=-1)

    qa = _rope(qa.reshape(bsz, s_len, A_HEADS, HEAD_DIM), positions)
    ka = _rope(ka.reshape(bsz, s_len, A_HEADS, HEAD_DIM), positions)
    va = va.reshape(bsz, s_len, A_HEADS, HEAD_DIM)
    iq = _rope(iq.reshape(bsz, s_len, IDX_HEADS, IDX_DIM), positions)
    ik = _rope(ik.reshape(bsz, s_len, 1, IDX_DIM), positions)[:, :, 0]
    ya = _dsa_attention(qa, ka, va, iq, ik, iw)

    qb = (_rmsnorm(cq, g_cq) @ w_uq).reshape(bsz, s_len, B_HEADS, B_NOPE + B_ROPE)
    qb = jnp.concatenate([qb[..., :B_NOPE], _rope(qb[..., B_NOPE:], positions)], axis=-1)
    kvb = (_rmsnorm(ckv, g_ckv) @ w_ukv).reshape(bsz, s_len, B_HEADS, B_NOPE + B_V)
    kr = _rope(kr.reshape(bsz, s_len, 1, B_ROPE), positions)
    kb = jnp.concatenate([kvb[..., :B_NOPE], jnp.broadcast_to(kr, (bsz, s_len, B_HEADS, B_ROPE))], axis=-1)
    yb = _dense_attention(qb, kb, kvb[..., B_NOPE:])

    log_f = jax.nn.log_sigmoid(fl.astype(jnp.float32) + b_f.astype(jnp.float32))
    cum = jnp.cumsum(log_f, axis=1)
    yc = _dense_attention(qc.reshape(bsz, s_len, C_HEADS, HEAD_DIM),
                          kc.reshape(bsz, s_len, C_HEADS, HEAD_DIM),
                          vc.reshape(bsz, s_len, C_HEADS, HEAD_DIM),
                          log_decay_cum=cum)

    y = jnp.concatenate([
        _rmsnorm(ya.reshape(bsz, s_len, A_WIDTH), g_out_a),
        _rmsnorm(yb.reshape(bsz, s_len, B_WIDTH), g_out_b),
        _rmsnorm(yc.reshape(bsz, s_len, C_WIDTH), g_out_c)], axis=-1)
    return y @ w_out


def setup_inputs(seed: int = 0) -> dict:
    key = jax.random.key(seed)
    ks = jax.random.split(key, 24)
    f32 = jnp.float32

    def nrm(k, shape, fan_in):
        return jax.random.normal(k, shape, f32) * fan_in ** -0.5

    def gain(k, shape):
        return 1.0 + 0.02 * jax.random.normal(k, shape, f32)

    x = jax.random.normal(ks[0], (BATCH, SEQ, D_MODEL), f32)
    c = jax.random.normal(ks[1], (BATCH, D_MODEL), f32)
    offset = jax.random.randint(ks[2], (BATCH, 1), 0, 64, dtype=jnp.int32) * CHUNK
    positions = offset + jnp.arange(SEQ, dtype=jnp.int32)[None, :]
    return {
        'x': x,
        'c': c,
        'positions': positions,
        'w_ada': nrm(ks[3], (DEPTH, D_MODEL, N_MOD * D_MODEL), D_MODEL),
        'b_ada': 0.02 * jax.random.normal(ks[4], (DEPTH, N_MOD * D_MODEL), f32),
        'g_attn': gain(ks[5], (DEPTH, D_MODEL)),
        'w_in': nrm(ks[6], (DEPTH, D_MODEL, IN_WIDTH), D_MODEL),
        'g_cq': gain(ks[7], (DEPTH, B_QLORA)),
        'g_ckv': gain(ks[8], (DEPTH, B_KVLORA)),
        'w_uq': nrm(ks[9], (DEPTH, B_QLORA, B_HEADS * (B_NOPE + B_ROPE)), B_QLORA),
        'w_ukv': nrm(ks[10], (DEPTH, B_KVLORA, B_HEADS * (B_NOPE + B_V)), B_KVLORA),
        'b_f': 2.0 + 2.0 * jax.random.uniform(ks[11], (DEPTH, C_HEADS), f32),
        'g_out_a': gain(ks[12], (DEPTH, A_WIDTH)),
        'g_out_b': gain(ks[13], (DEPTH, B_WIDTH)),
        'g_out_c': gain(ks[14], (DEPTH, C_WIDTH)),
        'w_out': nrm(ks[15], (DEPTH, MIX_WIDTH, D_MODEL), MIX_WIDTH),
        'g_mlp': gain(ks[16], (DEPTH, D_MODEL)),
        'w_up': nrm(ks[17], (DEPTH, D_MODEL, D_FF), D_MODEL),
        'w_down': nrm(ks[18], (DEPTH, D_FF, D_MODEL), D_FF),
        'g_final': gain(ks[19], (D_MODEL,)),
    }


def reference(x, c, positions, w_ada, b_ada, g_attn, w_in, g_cq, g_ckv, w_uq, w_ukv, b_f,
              g_out_a, g_out_b, g_out_c, w_out, g_mlp, w_up, w_down, g_final):
    cond = jax.nn.silu(c)
    for l in range(DEPTH):
        mod = (cond @ w_ada[l] + b_ada[l])[:, None, :]
        sh1, sc1, gt1, sh2, sc2, gt2 = jnp.split(mod, N_MOD, axis=-1)
        h = _rmsnorm(x, g_attn[l]) * (1 + sc1) + sh1
        x = x + gt1 * _mixer(h, positions, w_in[l], g_cq[l], g_ckv[l], w_uq[l], w_ukv[l], b_f[l],
                             g_out_a[l], g_out_b[l], g_out_c[l], w_out[l])
        h = _rmsnorm(x, g_mlp[l]) * (1 + sc2) + sh2
        x = x + gt2 * (jnp.square(jax.nn.relu(h @ w_up[l])) @ w_down[l])
    return _rmsnorm(x, g_final)
```

```python
import functools
import math

import numpy as np
import jax
import jax.numpy as jnp
from jax import lax
from jax.experimental import pallas as pl
from jax.experimental.pallas import tpu as pltpu

F32 = jnp.float32
BF16 = jnp.bfloat16

CHUNK = 64
ROPE_THETA = 10000.0
EPS = 1e-6
NEG = -1e30
HEAD_DIM = 128
IDX_HEADS = 32
IDX_DIM = 64
TOPK_MAX = 256
B_NOPE = 128
B_ROPE = 64
B_V = 128
N_MOD = 6

LANES = 128
SUBLANES = 8
KEY_TILE = 256
VMEM_LIMIT = 56 * 1024 * 1024
INT_MIN = -2147483648


def _params(sem):
    return pltpu.CompilerParams(dimension_semantics=sem, vmem_limit_bytes=VMEM_LIMIT)


def _ada_kernel(c_ref, w_ref, b_ref, o_ref):
    c = c_ref[...]
    cond = (c * jax.nn.sigmoid(c)).astype(BF16)
    acc = jnp.dot(cond, w_ref[...].astype(BF16), preferred_element_type=F32)
    o_ref[...] = acc + b_ref[...]


def _ada(c_pad, w_ada, b_ada):
    depth, d, n = w_ada.shape
    rows = c_pad.shape[0]
    tn = min(512, n)
    return pl.pallas_call(
        _ada_kernel,
        grid=(depth, n // tn),
        in_specs=[
            pl.BlockSpec((rows, d), lambda l, j: (0, 0)),
            pl.BlockSpec((None, d, tn), lambda l, j: (l, 0, j)),
            pl.BlockSpec((None, 1, tn), lambda l, j: (l, 0, j)),
        ],
        out_specs=pl.BlockSpec((None, rows, tn), lambda l, j: (l, 0, j)),
        out_shape=jax.ShapeDtypeStruct((depth, rows, n), F32),
        compiler_params=_params(("arbitrary", "arbitrary")),
    )(c_pad, w_ada, b_ada.reshape(depth, 1, n))


def _rope_table_kernel(pos_ref, c128_ref, s128_ref, c64_ref, s64_ref):
    pos = pos_ref[...]
    lane = lax.broadcasted_iota(jnp.int32, (1, LANES), 1)
    sign = jnp.where(lane < LANES // 2, -1.0, 1.0).astype(F32)
    for d, c_ref, s_ref in ((HEAD_DIM, c128_ref, s128_ref), (IDX_DIM, c64_ref, s64_ref)):
        f = (lane & (d // 2 - 1)) * 2
        inv = jnp.exp(f.astype(F32) * (-math.log(ROPE_THETA) / d))
        ang = pos * inv
        c_ref[...] = jnp.cos(ang)
        s_ref[...] = jnp.sin(ang) * sign


def _rope_tables(pos_col):
    t = pos_col.shape[0]
    ts = min(1024, t)
    spec = pl.BlockSpec((ts, LANES), lambda i: (i, 0))
    shp = jax.ShapeDtypeStruct((t, LANES), F32)
    return pl.pallas_call(
        _rope_table_kernel,
        grid=(t // ts,),
        in_specs=[pl.BlockSpec((ts, 1), lambda i: (i, 0))],
        out_specs=[spec] * 4,
        out_shape=[shp] * 4,
        compiler_params=_params(("arbitrary",)),
    )(pos_col)


def _modnorm_kernel(x_ref, g_ref, sc_ref, sh_ref, o_ref):
    x = x_ref[...]
    y = x * lax.rsqrt(jnp.mean(x * x, axis=-1, keepdims=True) + EPS) * g_ref[...]
    o_ref[...] = (y * (1.0 + sc_ref[...]) + sh_ref[...]).astype(o_ref.dtype)


def _rmsnorm_kernel(x_ref, g_ref, o_ref):
    x = x_ref[...]
    y = x * lax.rsqrt(jnp.mean(x * x, axis=-1, keepdims=True) + EPS) * g_ref[...]
    o_ref[...] = y.astype(o_ref.dtype)


def _modnorm(x2, g, sc, sh, seq, out_dtype):
    t, d = x2.shape
    ts = min(256, seq)
    row = pl.BlockSpec((ts, d), lambda i: (i, 0))
    vec = pl.BlockSpec((1, d), lambda i: (0, 0))
    if sc is None:
        return pl.pallas_call(
            _rmsnorm_kernel, grid=(t // ts,), in_specs=[row, vec], out_specs=row,
            out_shape=jax.ShapeDtypeStruct((t, d), out_dtype),
            compiler_params=_params(("arbitrary",)),
        )(x2, g.reshape(1, d))
    per_b = pl.BlockSpec((None, 1, d), lambda i: (i * ts // seq, 0, 0))
    return pl.pallas_call(
        _modnorm_kernel, grid=(t // ts,), in_specs=[row, vec, per_b, per_b], out_specs=row,
        out_shape=jax.ShapeDtypeStruct((t, d), out_dtype),
        compiler_params=_params(("arbitrary",)),
    )(x2, g.reshape(1, d), sc, sh)


def _mm_kernel(*refs, nk, n_extra, epilogue):
    a_ref, w_ref = refs[0], refs[1]
    extra = refs[2:2 + n_extra]
    o_ref = refs[2 + n_extra]
    if nk == 1:
        acc = jnp.dot(a_ref[...], w_ref[...], preferred_element_type=F32)
        epilogue(acc, o_ref, *extra)
        return
    acc_ref = refs[3 + n_extra]
    k = pl.program_id(2)

    @pl.when(k == 0)
    def _():
        acc_ref[...] = jnp.zeros_like(acc_ref)

    acc_ref[...] += jnp.dot(a_ref[...], w_ref[...], preferred_element_type=F32)

    @pl.when(k == nk - 1)
    def _():
        epilogue(acc_ref[...], o_ref, *extra)


def _matmul(a, w, epilogue, out_dtype, *, tm, tn, tk, extras=(), extra_specs=()):
    m, kdim = a.shape
    n = w.shape[1]
    tm, tn, tk = min(tm, m), min(tn, n), min(tk, kdim)
    nk = kdim // tk
    scratch = [pltpu.VMEM((tm, tn), F32)] if nk > 1 else []
    return pl.pallas_call(
        functools.partial(_mm_kernel, nk=nk, n_extra=len(extras), epilogue=epilogue),
        grid=(m // tm, n // tn, nk),
        in_specs=[pl.BlockSpec((tm, tk), lambda i, j, k: (i, k)),
                  pl.BlockSpec((tk, tn), lambda i, j, k: (k, j))] + list(extra_specs),
        out_specs=pl.BlockSpec((tm, tn), lambda i, j, k: (i, j)),
        out_shape=jax.ShapeDtypeStruct((m, n), out_dtype),
        scratch_shapes=scratch,
        compiler_params=_params(("arbitrary", "arbitrary", "arbitrary")),
    )(a, w, *extras)


def _epi_plain(acc, o_ref):
    o_ref[...] = acc.astype(o_ref.dtype)


def _epi_relu2(acc, o_ref):
    r = jnp.maximum(acc, 0.0)
    o_ref[...] = (r * r).astype(o_ref.dtype)


def _epi_rope(acc, o_ref, cos_ref, sin_ref, *, n_rope):
    cos, sin = cos_ref[...], sin_ref[...]
    for g in range(acc.shape[1] // LANES):
        blk = acc[:, g * LANES:(g + 1) * LANES]
        if g < n_rope:
            blk = blk * cos + pltpu.roll(blk, LANES // 2, 1) * sin
        o_ref[:, g * LANES:(g + 1) * LANES] = blk.astype(o_ref.dtype)


def _epi_rmsnorm(acc, o_ref, g_ref):
    y = acc * lax.rsqrt(jnp.mean(acc * acc, axis=-1, keepdims=True) + EPS) * g_ref[...]
    o_ref[...] = y.astype(o_ref.dtype)


def _epi_residual(acc, o_ref, x_ref, gate_ref):
    o_ref[...] = x_ref[...] + gate_ref[...] * acc


def _gate_kernel(z_ref, b_ref, o_ref):
    z = z_ref[...] + b_ref[...]
    x = jnp.minimum(z, 0.0) - jnp.log1p(jnp.exp(-jnp.abs(z)))
    n = x.shape[0]
    row = lax.broadcasted_iota(jnp.int32, x.shape, 0)
    k = 1
    while k < n:
        x = x + jnp.where(row >= k, pltpu.roll(x, k, 0), 0.0)
        k *= 2
    o_ref[...] = x


def _gate_cumsum(small, b_row, batch, seq, col_block):
    return pl.pallas_call(
        _gate_kernel,
        grid=(batch,),
        in_specs=[pl.BlockSpec((seq, LANES), lambda b: (b, col_block)),
                  pl.BlockSpec((1, LANES), lambda b: (0, 0))],
        out_specs=pl.BlockSpec((seq, LANES), lambda b: (b, 0)),
        out_shape=jax.ShapeDtypeStruct((batch * seq, LANES), F32),
        compiler_params=_params(("arbitrary",)),
    )(small, b_row)


def _vt_kernel(v_ref, o_ref, *, heads):
    for h in range(heads):
        blk = v_ref[:, h * HEAD_DIM:(h + 1) * HEAD_DIM].astype(F32)
        o_ref[h] = blk.T.astype(o_ref.dtype)


def _v_transpose(v_src, col_block0, batch, seq, heads):
    nkt = seq // KEY_TILE
    width_blocks = heads * HEAD_DIM
    return pl.pallas_call(
        functools.partial(_vt_kernel, heads=heads),
        grid=(batch, nkt),
        in_specs=[pl.BlockSpec((KEY_TILE, width_blocks), lambda b, t: (b * nkt + t, col_block0))],
        out_specs=pl.BlockSpec((None, heads, None, HEAD_DIM, KEY_TILE), lambda b, t: (b, 0, t, 0, 0)),
        out_shape=jax.ShapeDtypeStruct((batch, heads, nkt, HEAD_DIM, KEY_TILE), BF16),
        compiler_params=_params(("arbitrary", "arbitrary")),
    )(v_src)


def _dsa_kernel(iq_ref, ik_ref, iw_ref, q_ref, k_ref, vt_ref, g_ref, o_ref,
                rhs_ref, iwt_ref, key_ref, lg_ref, yt_ref, *, heads, k_sel):
    i = pl.program_id(1)
    tq = LANES
    n_kt = (i * tq + tq + KEY_TILE - 1) // KEY_TILE
    idx_scale = IDX_DIM ** -0.5 * IDX_HEADS ** -0.5
    scale = HEAD_DIM ** -0.5
    groups = KEY_TILE // SUBLANES

    feat = lax.broadcasted_iota(jnp.int32, (LANES, tq), 0)
    parity = (feat >> 5) & 1
    for p in range(IDX_HEADS // 2):
        t = iq_ref[:, p * LANES:(p + 1) * LANES].astype(F32).T
        rhs_ref[p, :, 0:tq] = jnp.where(parity == 0, t, 0.0).astype(BF16)
        rhs_ref[p, :, tq:2 * tq] = jnp.where(parity == 1, t, 0.0).astype(BF16)
    iwt_ref[...] = iw_ref[...].T * idx_scale

    q_pos = i * tq + lax.broadcasted_iota(jnp.int32, (KEY_TILE, tq), 1)
    k_off = lax.broadcasted_iota(jnp.int32, (KEY_TILE, tq), 0)

    def index_tile(kt, carry):
        off = pl.multiple_of(kt * KEY_TILE, KEY_TILE)
        ik_t = ik_ref[pl.ds(off, KEY_TILE), :].astype(BF16)
        acc = jnp.zeros((KEY_TILE, tq), F32)
        for p in range(IDX_HEADS // 2):
            rel = jnp.maximum(jnp.dot(ik_t, rhs_ref[p], preferred_element_type=F32), 0.0)
            acc = acc + rel[:, 0:tq] * iwt_ref[2 * p:2 * p + 1, :]
            acc = acc + rel[:, tq:2 * tq] * iwt_ref[2 * p + 1:2 * p + 2, :]
        u = pltpu.bitcast(acc, jnp.int32)
        key = u ^ ((u >> 31) & 0x7FFFFFFF)
        adm = ((k_off + off) >> 6) <= (q_pos >> 6)
        key_ref[pl.ds(off, KEY_TILE), :] = jnp.where(adm, key, INT_MIN)
        return carry

    lax.fori_loop(0, n_kt, index_tile, 0)

    def count_ge(cand):
        def body(kt, c):
            off = pl.multiple_of(kt * KEY_TILE, KEY_TILE)
            hit = (key_ref[pl.ds(off, KEY_TILE), :] >= cand).astype(jnp.int32)
            return c + jnp.sum(hit.reshape(groups, SUBLANES, tq), axis=0)
        c = lax.fori_loop(0, n_kt, body, jnp.zeros((SUBLANES, tq), jnp.int32))
        return jnp.sum(c, axis=0, keepdims=True)

    def select_bit(j, ans):
        cand = ans + jnp.left_shift(jnp.int32(1), 31 - j)
        return jnp.where(count_ge(cand) >= k_sel, cand, ans)

    ans = lax.fori_loop(0, 32, select_bit, jnp.full((1, tq), INT_MIN, jnp.int32))
    thr = jnp.maximum(ans, INT_MIN + 1)

    for h in range(heads):
        lo, hi = h * HEAD_DIM, (h + 1) * HEAD_DIM
        q_t = q_ref[:, lo:hi].astype(F32).T.astype(BF16)

        def logits_tile(kt, m, lo=lo, hi=hi, q_t=q_t):
            off = pl.multiple_of(kt * KEY_TILE, KEY_TILE)
            l = jnp.dot(k_ref[pl.ds(off, KEY_TILE), lo:hi], q_t, preferred_element_type=F32) * scale
            l = jnp.where(key_ref[pl.ds(off, KEY_TILE), :] >= thr, l, NEG)
            lg_ref[pl.ds(off, KEY_TILE), :] = l
            return jnp.maximum(m, jnp.max(l.reshape(groups, SUBLANES, tq), axis=0))

        m8 = lax.fori_loop(0, n_kt, logits_tile, jnp.full((SUBLANES, tq), NEG, F32))
        m = jnp.max(m8, axis=0, keepdims=True)

        def value_tile(kt, carry, h=h, m=m):
            den, acc = carry
            off = pl.multiple_of(kt * KEY_TILE, KEY_TILE)
            p = jnp.exp(lg_ref[pl.ds(off, KEY_TILE), :] - m)
            den = den + jnp.sum(p.reshape(groups, SUBLANES, tq), axis=0)
            acc = acc + jnp.dot(vt_ref[h, kt], p.astype(BF16), preferred_element_type=F32)
            return den, acc

        den8, acc = lax.fori_loop(
            0, n_kt, value_tile,
            (jnp.zeros((SUBLANES, tq), F32), jnp.zeros((HEAD_DIM, tq), F32)))
        yt_ref[lo:hi, :] = acc / jnp.sum(den8, axis=0, keepdims=True)

    y = yt_ref[...].T
    y = y * lax.rsqrt(jnp.mean(y * y, axis=-1, keepdims=True) + EPS) * g_ref[...]
    o_ref[...] = y.astype(o_ref.dtype)


def _dsa(iq, small, qk, vt, g, batch, seq, heads, k_sel, ik_block, iw_block):
    tq = LANES
    nq = seq // tq
    width = heads * HEAD_DIM
    nkt = seq // KEY_TILE
    return pl.pallas_call(
        functools.partial(_dsa_kernel, heads=heads, k_sel=k_sel),
        grid=(batch, nq),
        in_specs=[
            pl.BlockSpec((tq, IDX_HEADS * IDX_DIM), lambda b, i: (b * nq + i, 0)),
            pl.BlockSpec((seq, LANES), lambda b, i: (b, ik_block)),
            pl.BlockSpec((tq, LANES), lambda b, i: (b * nq + i, iw_block)),
            pl.BlockSpec((tq, width), lambda b, i: (b * nq + i, 0)),
            pl.BlockSpec((seq, width), lambda b, i: (b, 1)),
            pl.BlockSpec((None, heads, nkt, HEAD_DIM, KEY_TILE), lambda b, i: (b, 0, 0, 0, 0)),
            pl.BlockSpec((1, width), lambda b, i: (0, 0)),
        ],
        out_specs=pl.BlockSpec((tq, width), lambda b, i: (b * nq + i, 0)),
        out_shape=jax.ShapeDtypeStruct((batch * seq, width), BF16),
        scratch_shapes=[
            pltpu.VMEM((IDX_HEADS // 2, LANES, 2 * tq), BF16),
            pltpu.VMEM((LANES, tq), F32),
            pltpu.VMEM((seq, tq), jnp.int32),
            pltpu.VMEM((seq, tq), F32),
            pltpu.VMEM((width, tq), F32),
        ],
        compiler_params=_params(("arbitrary", "arbitrary")),
    )(iq, small, small, qk, qk, vt, g.reshape(1, width))


def _dense_attn_kernel(*refs, mode, tq, scale):
    if mode == "mla":
        qn_ref, qr_ref, kn_ref, kr_ref, v_ref, o_ref, lg_ref = refs
    else:
        q_ref, k_ref, v_ref, cq_ref, ck_ref, o_ref, lg_ref = refs
    i = pl.program_id(2)
    n_kt = (i * tq + tq + KEY_TILE - 1) // KEY_TILE
    nt = (((1,), (1,)), ((), ()))
    t_pos = i * tq + lax.broadcasted_iota(jnp.int32, (tq, KEY_TILE), 0)
    s_off = lax.broadcasted_iota(jnp.int32, (tq, KEY_TILE), 1)

    if mode == "mla":
        par = pl.program_id(1) & 1
        lane = lax.broadcasted_iota(jnp.int32, (tq, LANES), 1)
        qr = jnp.where(((lane >> 5) & 1) == par, qr_ref[...], jnp.zeros_like(qr_ref[...]))
        q = jnp.concatenate([qn_ref[...], qr], axis=1)
    else:
        q = q_ref[...]
        cq = cq_ref[...]

    def logits_tile(kt, m):
        off = pl.multiple_of(kt * KEY_TILE, KEY_TILE)
        if mode == "mla":
            k_t = jnp.concatenate([kn_ref[pl.ds(off, KEY_TILE), :],
                                   kr_ref[pl.ds(off, KEY_TILE), :].astype(BF16)], axis=1)
            l = lax.dot_general(q, k_t, nt, preferred_element_type=F32) * scale
            mask = ((s_off + off) >> 6) <= (t_pos >> 6)
        else:
            l = lax.dot_general(q, k_ref[pl.ds(off, KEY_TILE), :], nt, preferred_element_type=F32) * scale
            l = l + cq - ck_ref[kt]
            mask = (s_off + off) <= t_pos
        l = jnp.where(mask, l, NEG)
        lg_ref[kt] = l
        return jnp.maximum(m, jnp.maximum(l[:, 0:LANES], l[:, LANES:2 * LANES]))

    m128 = lax.fori_loop(0, n_kt, logits_tile, jnp.full((tq, LANES), NEG, F32))
    m = jnp.max(m128, axis=1, keepdims=True)

    def value_tile(kt, carry):
        den, acc = carry
        off = pl.multiple_of(kt * KEY_TILE, KEY_TILE)
        p = jnp.exp(lg_ref[kt] - m)
        den = den + p[:, 0:LANES] + p[:, LANES:2 * LANES]
        acc = acc + jnp.dot(p.astype(BF16), v_ref[pl.ds(off, KEY_TILE), :], preferred_element_type=F32)
        return den, acc

    den128, acc = lax.fori_loop(
        0, n_kt, value_tile, (jnp.zeros((tq, LANES), F32), jnp.zeros((tq, HEAD_DIM), F32)))
    o_ref[...] = (acc / jnp.sum(den128, axis=1, keepdims=True)).astype(o_ref.dtype)


def _mla(qn, qr, kv, small, batch, seq, heads, kr_block):
    tq = min(256, seq)
    nq = seq // tq
    nkt = seq // KEY_TILE
    scale = (B_NOPE + B_ROPE) ** -0.5
    return pl.pallas_call(
        functools.partial(_dense_attn_kernel, mode="mla", tq=tq, scale=scale),
        grid=(batch, heads, nq),
        in_specs=[
            pl.BlockSpec((tq, LANES), lambda b, h, i: (b * nq + i, h)),
            pl.BlockSpec((tq, LANES), lambda b, h, i: (b * nq + i, h // 2)),
            pl.BlockSpec((seq, LANES), lambda b, h, i: (b, h)),
            pl.BlockSpec((seq, LANES), lambda b, h, i: (b, kr_block)),
            pl.BlockSpec((seq, LANES), lambda b, h, i: (b, heads + h)),
        ],
        out_specs=pl.BlockSpec((tq, LANES), lambda b, h, i: (b * nq + i, h)),
        out_shape=jax.ShapeDtypeStruct((batch * seq, heads * B_V), F32),
        scratch_shapes=[pltpu.VMEM((nkt, tq, KEY_TILE), F32)],
        compiler_params=_params(("arbitrary", "arbitrary", "arbitrary")),
    )(qn, qr, kv, small, kv)


def _fox(plain, cum_q, cum_k, batch, seq, heads, q_block0, k_block0, v_block0):
    tq = min(256, seq)
    nq = seq // tq
    nkt = seq // KEY_TILE
    scale = HEAD_DIM ** -0.5
    return pl.pallas_call(
        functools.partial(_dense_attn_kernel, mode="fox", tq=tq, scale=scale),
        grid=(batch, heads, nq),
        in_specs=[
            pl.BlockSpec((tq, LANES), lambda b, h, i: (b * nq + i, q_block0 + h)),
            pl.BlockSpec((seq, LANES), lambda b, h, i: (b, k_block0 + h)),
            pl.BlockSpec((seq, LANES), lambda b, h, i: (b, v_block0 + h)),
            pl.BlockSpec((None, None, tq, 1), lambda b, h, i: (b, h, i, 0)),
            pl.BlockSpec((None, None, nkt, 1, KEY_TILE), lambda b, h, i: (b, h, 0, 0, 0)),
        ],
        out_specs=pl.BlockSpec((tq, LANES), lambda b, h, i: (b * nq + i, h)),
        out_shape=jax.ShapeDtypeStruct((batch * seq, heads * HEAD_DIM), F32),
        scratch_shapes=[pltpu.VMEM((nkt, tq, KEY_TILE), F32)],
        compiler_params=_params(("arbitrary", "arbitrary", "arbitrary")),
    )(plain, plain, plain, cum_q, cum_k)


def _concat_norm_kernel(ya_ref, yb_ref, yc_ref, gb_ref, gc_ref, o_ref):
    wa, wb = ya_ref.shape[1], yb_ref.shape[1]
    o_ref[:, 0:wa] = ya_ref[...]
    yb = yb_ref[...]
    yb = yb * lax.rsqrt(jnp.mean(yb * yb, axis=-1, keepdims=True) + EPS) * gb_ref[...]
    o_ref[:, wa:wa + wb] = yb.astype(o_ref.dtype)
    yc = yc_ref[...]
    yc = yc * lax.rsqrt(jnp.mean(yc * yc, axis=-1, keepdims=True) + EPS) * gc_ref[...]
    o_ref[:, wa + wb:] = yc.astype(o_ref.dtype)


def _concat_norm(ya, yb, yc, gb, gc):
    t = ya.shape[0]
    wa, wb, wc = ya.shape[1], yb.shape[1], yc.shape[1]
    ts = min(256, t)
    return pl.pallas_call(
        _concat_norm_kernel,
        grid=(t // ts,),
        in_specs=[pl.BlockSpec((ts, wa), lambda i: (i, 0)),
                  pl.BlockSpec((ts, wb), lambda i: (i, 0)),
                  pl.BlockSpec((ts, wc), lambda i: (i, 0)),
                  pl.BlockSpec((1, wb), lambda i: (0, 0)),
                  pl.BlockSpec((1, wc), lambda i: (0, 0))],
        out_specs=pl.BlockSpec((ts, wa + wb + wc), lambda i: (i, 0)),
        out_shape=jax.ShapeDtypeStruct((t, wa + wb + wc), BF16),
        compiler_params=_params(("arbitrary",)),
    )(ya, yb, yc, gb.reshape(1, wb), gc.reshape(1, wc))


def _pair_cols(w, heads):
    k = w.shape[0]
    w = w.reshape(k, heads // 2, 2, 2, IDX_DIM // 2)
    return jnp.swapaxes(w, 2, 3).reshape(k, heads * IDX_DIM)


def _dup_cols(w):
    a, b = w[:, :IDX_DIM // 2], w[:, IDX_DIM // 2:]
    return jnp.concatenate([a, a, b, b], axis=1)


def kernel(x, c, positions, w_ada, b_ada, g_attn, w_in, g_cq, g_ckv, w_uq, w_ukv, b_f,
           g_out_a, g_out_b, g_out_c, w_out, g_mlp, w_up, w_down, g_final):
    batch, seq, d = x.shape
    depth = w_ada.shape[0]
    t = batch * seq
    a_heads = g_out_a.shape[1] // HEAD_DIM
    b_heads = g_out_b.shape[1] // B_V
    c_heads = g_out_c.shape[1] // HEAD_DIM
    a_w, c_w = a_heads * HEAD_DIM, c_heads * HEAD_DIM
    q_lora, kv_lora = g_cq.shape[1], g_ckv.shape[1]
    iq_w = IDX_HEADS * IDX_DIM
    k_sel = min(TOPK_MAX, seq // 4)
    splits = (a_w, a_w, a_w, iq_w, IDX_DIM, IDX_HEADS, q_lora, kv_lora, B_ROPE, c_w, c_w, c_w, c_heads)
    offs = np.concatenate([[0], np.cumsum(splits)]).tolist()

    rows = SUBLANES
    c_pad = jnp.zeros((rows, d), F32).at[:batch].set(c)
    mod = _ada(c_pad, w_ada, b_ada)[:, :batch]
    cos128, sin128, cos64, sin64 = _rope_tables(positions.astype(F32).reshape(t, 1))

    tab128 = (cos128, sin128)
    tab64 = (cos64, sin64)
    x2 = x.reshape(t, d)
    tm = min(1024, seq)

    def tab_specs():
        s = pl.BlockSpec((tm, LANES), lambda i, j, k: (i, 0))
        return [s, s]

    for l in range(depth):
        sh1, sc1, gt1, sh2, sc2, gt2 = [m.reshape(batch, 1, d) for m in jnp.split(mod[l], N_MOD, axis=-1)]
        wl = w_in[l]
        seg = [wl[:, offs[n]:offs[n + 1]] for n in range(len(splits))]
        w_qk = jnp.concatenate([seg[0], seg[1]], axis=1).astype(BF16)
        w_plain = jnp.concatenate([seg[2], seg[9], seg[10], seg[11]], axis=1).astype(BF16)
        w_iq = _pair_cols(seg[3], IDX_HEADS).astype(BF16)
        pad = LANES - IDX_HEADS - c_heads
        w_small = jnp.concatenate(
            [_dup_cols(seg[8]), _dup_cols(seg[4]), seg[5], seg[12], jnp.zeros((d, pad), F32)], axis=1).astype(BF16)
        w_cq = seg[6].astype(BF16)
        w_ckv = seg[7].astype(BF16)
        wq = w_uq[l].reshape(q_lora, b_heads, B_NOPE + B_ROPE)
        w_qn = wq[:, :, :B_NOPE].reshape(q_lora, b_heads * B_NOPE).astype(BF16)
        w_qr = _pair_cols(wq[:, :, B_NOPE:].reshape(q_lora, b_heads * B_ROPE), b_heads).astype(BF16)
        wkv = w_ukv[l].reshape(kv_lora, b_heads, B_NOPE + B_V)
        w_kv = jnp.concatenate([wkv[:, :, :B_NOPE].reshape(kv_lora, -1),
                                wkv[:, :, B_NOPE:].reshape(kv_lora, -1)], axis=1).astype(BF16)

        h = _modnorm(x2, g_attn[l], sc1, sh1, seq, BF16)

        rope_all = functools.partial(_epi_rope, n_rope=1 << 30)
        qk = _matmul(h, w_qk, rope_all, BF16, tm=tm, tn=512, tk=d, extras=tab128, extra_specs=tab_specs())
        plain = _matmul(h, w_plain, _epi_plain, BF16, tm=tm, tn=512, tk=d)
        iq = _matmul(h, w_iq, rope_all, BF16, tm=tm, tn=512, tk=d, extras=tab64, extra_specs=tab_specs())
        small = _matmul(h, w_small, functools.partial(_epi_rope, n_rope=2), F32, tm=tm, tn=3 * LANES, tk=d,
                        extras=tab64, extra_specs=tab_specs())
        cqn = _matmul(h, w_cq, _epi_rmsnorm, BF16, tm=tm, tn=q_lora, tk=d, extras=(g_cq[l].reshape(1, -1),),
                      extra_specs=[pl.BlockSpec((1, q_lora), lambda i, j, k: (0, 0))])
        ckvn = _matmul(h, w_ckv, _epi_rmsnorm, BF16, tm=tm, tn=kv_lora, tk=d, extras=(g_ckv[l].reshape(1, -1),),
                       extra_specs=[pl.BlockSpec((1, kv_lora), lambda i, j, k: (0, 0))])
        qn = _matmul(cqn, w_qn, _epi_plain, BF16, tm=tm, tn=512, tk=q_lora)
        qr = _matmul(cqn, w_qr, rope_all, BF16, tm=tm, tn=512, tk=q_lora, extras=tab64, extra_specs=tab_specs())
        kv = _matmul(ckvn, w_kv, _epi_plain, BF16, tm=tm, tn=512, tk=kv_lora)

        vt = _v_transpose(plain, 0, batch, seq, a_heads)
        ya = _dsa(iq, small, qk, vt, g_out_a[l], batch, seq, a_heads, k_sel, ik_block=1, iw_block=2)

        yb = _mla(qn, qr, kv, small, batch, seq, b_heads, kr_block=0)

        b_row = jnp.zeros((1, LANES), F32).at[0, IDX_HEADS:IDX_HEADS + c_heads].set(b_f[l])
        cum = _gate_cumsum(small, b_row, batch, seq, col_block=2)
        cum = cum.reshape(batch, seq, LANES)[:, :, IDX_HEADS:IDX_HEADS + c_heads]
        cum_t = jnp.swapaxes(cum, 1, 2)
        cum_q = cum_t.reshape(batch, c_heads, seq, 1)
        cum_k = cum_t.reshape(batch, c_heads, seq // KEY_TILE, 1, KEY_TILE)
        hb = a_heads
        yc = _fox(plain, cum_q, cum_k, batch, seq, c_heads,
                  q_block0=hb, k_block0=hb + c_heads, v_block0=hb + 2 * c_heads)

        y = _concat_norm(ya, yb, yc, g_out_b[l], g_out_c[l])

        res_specs = [pl.BlockSpec((tm, 512), lambda i, j, k: (i, j)),
                     pl.BlockSpec((None, 1, 512), lambda i, j, k: (i * tm // seq, 0, j))]
        x2 = _matmul(y, w_out[l].astype(BF16), _epi_residual, F32, tm=tm, tn=512, tk=d,
                     extras=(x2, gt1), extra_specs=res_specs)

        h2 = _modnorm(x2, g_mlp[l], sc2, sh2, seq, BF16)
        u = _matmul(h2, w_up[l].astype(BF16), _epi_relu2, BF16, tm=tm, tn=512, tk=d)
        x2 = _matmul(u, w_down[l].astype(BF16), _epi_residual, F32, tm=tm, tn=512, tk=2048,
                     extras=(x2, gt2), extra_specs=res_specs)

    return _modnorm(x2, g_final, None, None, seq, F32).reshape(batch, seq, d)
```
